```python
import jax, jax.numpy as jnp
from jax import lax
import numpy as np

D_MODEL = 1024
BATCH = 8
SEQ = 8192
DEPTH = 2

N_META = 16
N_A_LAYERS = DEPTH // 2
N_B_LAYERS = DEPTH - N_A_LAYERS
POOL_WINDOWS = (2, 4, 8, 16)
POOL_GROUP = D_MODEL // len(POOL_WINDOWS)
SB_HEADS = 16
SB_HEAD_DIM = D_MODEL // SB_HEADS
Q_BLOCK = 128
PEER_HEADS = 8
PEER_N_KEYS = 128
PEER_N_EXPERTS = PEER_N_KEYS * PEER_N_KEYS
PEER_TOPK = 16
PEER_QUERY_DIM = 256
PEER_HALF = PEER_QUERY_DIM // 2
PEER_CHUNK = 16
EPS = 1e-6

kernel_name = "yoco_pool_stickbreak_peer"


def rmsnorm(x, g):
    xf = x.astype(jnp.float32)
    y = xf * lax.rsqrt(jnp.mean(xf * xf, axis=-1, keepdims=True) + EPS)
    return (y * g.astype(jnp.float32)).astype(x.dtype)


def pool_mixer(h, w_pool, scale):
    L = h.shape[1]
    hf = h.astype(jnp.float32)
    pos = jnp.arange(L)
    outs = []
    for g, w in enumerate(POOL_WINDOWS):
        hg = hf[..., g * POOL_GROUP:(g + 1) * POOL_GROUP]
        c = jnp.cumsum(hg, axis=1)
        c_prev = jnp.pad(c, ((0, 0), (w, 0), (0, 0)))[:, :L]
        cnt = jnp.minimum(pos + 1, w).astype(jnp.float32)[None, :, None]
        pooled = ((c - c_prev) / cnt - hg).astype(h.dtype)
        outs.append(jnp.einsum('bsc,cd->bsd', pooled, w_pool[g]))
    return jnp.concatenate(outs, axis=-1) * scale


def stick_breaking(q, k, v):
    B, L, H, dh = q.shape
    pad = (-L) % Q_BLOCK
    Lp = L + pad
    nblk = Lp // Q_BLOCK
    pw = ((0, 0), (pad, 0), (0, 0), (0, 0))
    qb = jnp.pad(q, pw).reshape(B, nblk, Q_BLOCK, H, dh).transpose(1, 0, 3, 2, 4)
    kp = jnp.pad(k, pw).transpose(0, 2, 1, 3)
    vp = jnp.pad(v, pw).transpose(0, 2, 1, 3)
    key_pos = jnp.arange(Lp)
    key_real = key_pos >= pad
    scale = SB_HEAD_DIM ** -0.5

    def block(args):
        q_blk, i = args
        q_pos = i * Q_BLOCK + jnp.arange(Q_BLOCK)
        z = jnp.einsum('bhqd,bhkd->bhqk', q_blk, kp,
                       preferred_element_type=jnp.float32) * scale
        valid = (key_pos[None, :] < q_pos[:, None]) & key_real[None, :]
        log_keep = jnp.where(valid, jax.nn.log_sigmoid(-z), 0.0)
        after = lax.cumsum(log_keep, axis=3, reverse=True) - log_keep
        log_w = jax.nn.log_sigmoid(z) + after
        wgt = jnp.where(valid, jnp.exp(log_w), 0.0)
        return jnp.einsum('bhqk,bhkd->bqhd', wgt.astype(v.dtype), vp)

    out = lax.map(block, (qb, jnp.arange(nblk)))
    out = out.transpose(1, 0, 2, 3, 4).reshape(B, Lp, H * dh)
    return out[:, pad:]


def peer(h, w_query, sub_keys, u, v):
    B, L, D = h.shape
    nchunk = L // PEER_CHUNK
    hc = h.reshape(B, nchunk, PEER_CHUNK, D).transpose(1, 0, 2, 3)

    def chunk(hx):
        q = jnp.einsum('btd,dq->btq', hx, w_query).reshape(
            B, PEER_CHUNK, PEER_HEADS, 2, PEER_HALF)
        s = jnp.einsum('bthpc,hpnc->bthpn', q, sub_keys,
                       preferred_element_type=jnp.float32)
        top_s, top_i = lax.top_k(s, PEER_TOPK)
        cand_s = (top_s[..., 0, :, None] + top_s[..., 1, None, :]).reshape(
            B, PEER_CHUNK, PEER_HEADS, PEER_TOPK * PEER_TOPK)
        cand_i = (top_i[..., 0, :, None] * PEER_N_KEYS + top_i[..., 1, None, :]).reshape(
            B, PEER_CHUNK, PEER_HEADS, PEER_TOPK * PEER_TOPK)
        best_s, best_pos = lax.top_k(cand_s, PEER_TOPK)
        expert = jnp.take_along_axis(cand_i, best_pos, axis=-1)
        gate = jax.nn.softmax(best_s, axis=-1)
        u_sel = jnp.take(u, expert, axis=0)
        v_sel = jnp.take(v, expert, axis=0)
        act = jax.nn.gelu(jnp.einsum('btd,bthkd->bthk', hx, u_sel,
                                     preferred_element_type=jnp.float32), approximate=False)
        return jnp.einsum('bthk,bthkd->btd', (gate * act).astype(h.dtype), v_sel)

    y = lax.map(chunk, hc)
    return y.transpose(1, 0, 2, 3).reshape(B, L, D)


def setup_inputs(seed: int = 0) -> dict:
    key = jax.random.key(seed)
    ks = jax.random.split(key, 16)
    f = jnp.float32
    D = D_MODEL
    nrm = lambda k, shape, s: jax.random.normal(k, shape, f) * s
    return {
        "x": jax.random.normal(ks[0], (BATCH, SEQ, D), f),
        "meta_tokens": nrm(ks[1], (N_META, D), 1.0),
        "norm_mix": 1.0 + nrm(ks[2], (DEPTH, D), 0.02),
        "norm_ffn": 1.0 + nrm(ks[3], (DEPTH, D), 0.02),
        "pool_w": nrm(ks[4], (N_A_LAYERS, len(POOL_WINDOWS), POOL_GROUP, POOL_GROUP), POOL_GROUP ** -0.5),
        "pool_scale": 0.5 + nrm(ks[5], (N_A_LAYERS, D), 0.05),
        "kv_norm": 1.0 + nrm(ks[6], (D,), 0.02),
        "w_kv": nrm(ks[7], (D, 2 * D), D ** -0.5),
        "k_norm": 1.0 + nrm(ks[8], (SB_HEAD_DIM,), 0.02),
        "w_q": nrm(ks[9], (N_B_LAYERS, D, D), D ** -0.5),
        "q_norm": 1.0 + nrm(ks[10], (N_B_LAYERS, SB_HEAD_DIM), 0.02),
        "w_o": nrm(ks[11], (N_B_LAYERS, D, D), D ** -0.5),
        "peer_wq": nrm(ks[12], (DEPTH, D, PEER_HEADS * PEER_QUERY_DIM), D ** -0.5),
        "peer_keys": nrm(ks[13], (DEPTH, PEER_HEADS, 2, PEER_N_KEYS, PEER_HALF), PEER_HALF ** -0.5),
        "peer_u": nrm(ks[14], (DEPTH, PEER_N_EXPERTS, D), D ** -0.5),
        "peer_v": nrm(ks[15], (DEPTH, PEER_N_EXPERTS, D), 0.25),
    }


def reference(x, meta_tokens, norm_mix, norm_ffn, pool_w, pool_scale, kv_norm, w_kv, k_norm,
              w_q, q_norm, w_o, peer_wq, peer_keys, peer_u, peer_v):
    B = x.shape[0]
    meta = jnp.broadcast_to(meta_tokens[None].astype(x.dtype), (B, N_META, D_MODEL))
    h = jnp.concatenate([meta, x], axis=1)
    L = h.shape[1]
    k_sh = None
    v_sh = None
    for layer in range(DEPTH):
        hn = rmsnorm(h, norm_mix[layer])
        if layer < N_A_LAYERS:
            mix = pool_mixer(hn, pool_w[layer], pool_scale[layer])
        else:
            j = layer - N_A_LAYERS
            q = jnp.einsum('bld,de->ble', hn, w_q[j]).reshape(B, L, SB_HEADS, SB_HEAD_DIM)
            q = rmsnorm(q, q_norm[j])
            mix = jnp.einsum('ble,ed->bld', stick_breaking(q, k_sh, v_sh), w_o[j])
        h = h + mix
        h = h + peer(rmsnorm(h, norm_ffn[layer]), peer_wq[layer], peer_keys[layer],
                     peer_u[layer], peer_v[layer])
        if layer == N_A_LAYERS - 1:
            kv = jnp.einsum('bld,de->ble', rmsnorm(h, kv_norm), w_kv)
            k_sh = rmsnorm(kv[..., :D_MODEL].reshape(B, L, SB_HEADS, SB_HEAD_DIM), k_norm)
            v_sh = kv[..., D_MODEL:].reshape(B, L, SB_HEADS, SB_HEAD_DIM)
    return h[:, N_META:]
```

```python
import functools

import jax
import jax.numpy as jnp
import numpy as np
from jax import lax
from jax.experimental import pallas as pl
from jax.experimental.pallas import tpu as pltpu

EPS = 1e-6
N_META = 16
POOL_WINDOWS = (2, 4, 8, 16)
SB_HEADS = 16
PEER_HEADS = 8
PEER_TOPK = 16
PEER_N_KEYS = 128

LANES = 128
SUBLANES = 8
ROW_TILE = 128
VMEM_TABLE_LIMIT = 56 * 1024 * 1024
VMEM_LIMIT = 48 * 1024 * 1024
STICK_EXIT = -90.0


def _seq_tile(lp):
    for t in (5 * ROW_TILE, 4 * ROW_TILE, 2 * ROW_TILE):
        if lp % t == 0:
            return t
    return ROW_TILE


def _rms(x, g):
    return x * lax.rsqrt(jnp.mean(x * x, axis=-1, keepdims=True) + EPS) * g


def _pool_kernel(h_ref, halo_ref, g_ref, w_ref, sc_ref, o_ref, hn_s, *, tile):
    i = pl.program_id(1)
    g = g_ref[...]
    h = h_ref[0]
    hn_s[N_META:, :] = _rms(h, g)
    halo = _rms(halo_ref[0], g)
    hn_s[0:N_META, :] = jnp.where(i == 0, 0.0, halo)
    pos = i * tile + lax.broadcasted_iota(jnp.int32, (tile, 1), 0)
    ngrp = len(POOL_WINDOWS)
    gw = h.shape[-1] // ngrp
    for gi, w in enumerate(POOL_WINDOWS):
        cols = slice(gi * gw, (gi + 1) * gw)
        cur = hn_s[N_META:N_META + tile, cols]
        acc = cur
        for k in range(1, w):
            acc = acc + hn_s[N_META - k:N_META - k + tile, cols]
        cnt = jnp.minimum(pos + 1, w).astype(jnp.float32)
        pooled = acc / cnt - cur
        mix = jnp.dot(pooled.astype(jnp.bfloat16), w_ref[gi],
                      preferred_element_type=jnp.float32)
        o_ref[0, :, cols] = h[:, cols] + mix * sc_ref[:, cols]


def _pool_layer(h, gain, w_bf, scale):
    B, Lp, D = h.shape
    tile = _seq_tile(Lp)
    nt = Lp // tile
    hb = tile // N_META
    return pl.pallas_call(
        functools.partial(_pool_kernel, tile=tile),
        grid=(B, nt),
        in_specs=[
            pl.BlockSpec((1, tile, D), lambda b, i: (b, i, 0)),
            pl.BlockSpec((1, N_META, D), lambda b, i: (b, jnp.maximum(i * hb - 1, 0), 0)),
            pl.BlockSpec((1, D), lambda b, i: (0, 0)),
            pl.BlockSpec(w_bf.shape, lambda b, i: (0, 0, 0)),
            pl.BlockSpec((1, D), lambda b, i: (0, 0)),
        ],
        out_specs=pl.BlockSpec((1, tile, D), lambda b, i: (b, i, 0)),
        out_shape=jax.ShapeDtypeStruct(h.shape, jnp.float32),
        scratch_shapes=[pltpu.VMEM((tile + N_META, D), jnp.float32)],
        compiler_params=pltpu.CompilerParams(
            dimension_semantics=("arbitrary", "arbitrary"), vmem_limit_bytes=VMEM_LIMIT),
        name="pool_mixer",
    )(h, h, gain.reshape(1, D), w_bf, scale.reshape(1, D))


def _topk_rows(s, k):
    n = s.shape[0]
    iota = lax.broadcasted_iota(jnp.int32, s.shape, 0)
    vals, idxs = [], []
    for _ in range(k):
        m = jnp.max(s, axis=0, keepdims=True)
        pos = jnp.min(jnp.where(s == m, iota, n), axis=0, keepdims=True)
        vals.append(m)
        idxs.append(pos)
        s = jnp.where(iota == pos, -jnp.inf, s)
    return vals, idxs


def _route_kernel(h_ref, g_ref, wqt_ref, keys_ref, hn_ref, idx_ref, gate_ref, qt_s):
    K = PEER_TOPK
    hn = _rms(h_ref[...], g_ref[...])
    hn_ref[...] = hn
    qt_s[...] = lax.dot_general(wqt_ref[...], hn.astype(jnp.bfloat16),
                                (((1,), (1,)), ((), ())),
                                preferred_element_type=jnp.float32)
    idx_rows, gate_rows = [], []
    for hd in range(PEER_HEADS):
        tops = []
        for p in range(2):
            r0 = (hd * 2 + p) * LANES
            st = jnp.dot(keys_ref[hd, p].astype(jnp.bfloat16),
                         qt_s[r0:r0 + LANES, :].astype(jnp.bfloat16),
                         preferred_element_type=jnp.float32)
            tops.append(_topk_rows(st, K))
        (s1, i1), (s2, i2) = tops
        s2_all = jnp.concatenate(s2, axis=0)
        cand = jnp.concatenate([s1[a] + s2_all for a in range(K)], axis=0)
        best_s, best_pos = _topk_rows(cand, K)
        bs = jnp.concatenate(best_s, axis=0)
        bp = jnp.concatenate(best_pos, axis=0)
        pa = bp >> 4
        pb = bp & (K - 1)
        e1 = jnp.zeros_like(bp)
        e2 = jnp.zeros_like(bp)
        for a in range(K):
            e1 = jnp.where(pa == a, i1[a], e1)
            e2 = jnp.where(pb == a, i2[a], e2)
        idx_rows.append(e1 * PEER_N_KEYS + e2)
        ex = jnp.exp(bs - best_s[0])
        gate_rows.append(ex / jnp.sum(ex, axis=0, keepdims=True))
    idx_ref[...] = jnp.concatenate(idx_rows, axis=0).T
    gate_ref[...] = jnp.concatenate(gate_rows, axis=0).T


def _peer_route(h2, gain, wqt_bf, keys, tile=ROW_TILE):
    T, D = h2.shape
    Q = wqt_bf.shape[0]
    npair = PEER_HEADS * PEER_TOPK
    return pl.pallas_call(
        _route_kernel,
        grid=(T // tile,),
        in_specs=[
            pl.BlockSpec((tile, D), lambda i: (i, 0)),
            pl.BlockSpec((1, D), lambda i: (0, 0)),
            pl.BlockSpec((Q, D), lambda i: (0, 0)),
            pl.BlockSpec(keys.shape, lambda i: (0, 0, 0, 0)),
        ],
        out_specs=[
            pl.BlockSpec((tile, D), lambda i: (i, 0)),
            pl.BlockSpec((tile, npair), lambda i: (i, 0)),
            pl.BlockSpec((tile, npair), lambda i: (i, 0)),
        ],
        out_shape=[
            jax.ShapeDtypeStruct((T, D), jnp.float32),
            jax.ShapeDtypeStruct((T, npair), jnp.int32),
            jax.ShapeDtypeStruct((T, npair), jnp.float32),
        ],
        scratch_shapes=[pltpu.VMEM((Q, tile), jnp.float32)],
        compiler_params=pltpu.CompilerParams(
            dimension_semantics=("arbitrary",), vmem_limit_bytes=VMEM_LIMIT),
        name="peer_route",
    )(h2, gain.reshape(1, D), wqt_bf, keys)


def _pack_table(tab):
    E, D = tab.shape
    bits = lax.bitcast_convert_type(tab.astype(jnp.bfloat16), jnp.uint16).astype(jnp.uint32)
    bits = bits.reshape(E // 2, 2, D)
    packed = bits[:, 0, :] | (bits[:, 1, :] << 16)
    return packed.reshape(E // 2, SUBLANES, D // SUBLANES)


def _expert_row(tab_ref, e):
    w = tab_ref[e >> 1]
    sh = ((e & 1) ^ 1) * 16
    return lax.bitcast_convert_type((w << sh.astype(jnp.uint32)) & jnp.uint32(0xFFFF0000),
                                    jnp.float32)


def _fold8(ps):
    sub = lax.broadcasted_iota(jnp.int32, (SUBLANES, LANES), 0)

    def half(x, d, up):
        return x + pltpu.roll(x, (SUBLANES - d) if up else d, 0)

    l1 = [jnp.where(sub < 4, half(ps[2 * i], 4, False), half(ps[2 * i + 1], 4, False))
          for i in range(4)]
    l2 = [jnp.where((sub & 2) != 0, half(l1[2 * i], 2, False), half(l1[2 * i + 1], 2, True))
          for i in range(2)]
    return jnp.where((sub & 1) != 0, half(l2[0], 1, False), half(l2[1], 1, True))


def _fold8_order():
    ps = [np.full((SUBLANES, 1), float(i)) for i in range(8)]
    sub = np.arange(SUBLANES)[:, None]
    l1 = [np.where(sub < 4, ps[2 * i], ps[2 * i + 1]) for i in range(4)]
    l2 = [np.where((sub & 2) != 0, l1[2 * i], l1[2 * i + 1]) for i in range(2)]
    l3 = np.where((sub & 1) != 0, l2[0], l2[1])
    return [int(v) for v in l3[:, 0]]


_FOLD_ORDER = _fold8_order()


def _gelu(x):
    return 0.5 * x * (1.0 + lax.erf(x * np.float32(1.0 / np.sqrt(2.0))))


def _peer_u_kernel(idx_ref, hn_ref, gate_ref, tab_ref, coef_ref, s_scr, *, tile):
    npair = PEER_HEADS * PEER_TOPK
    ones = jnp.ones((SUBLANES, LANES), jnp.float32)

    def token(t, carry):
        h = hn_ref[t]
        rows = []
        for g in range(npair // 8):
            prods = [_expert_row(tab_ref, idx_ref[t, g * 8 + j]) * h for j in range(8)]
            fed = [None] * 8
            for r in range(8):
                fed[_FOLD_ORDER[r]] = prods[r]
            rows.append(_fold8(fed))
        part = jnp.concatenate(rows, axis=0)
        s = lax.dot_general(ones, part, (((1,), (1,)), ((), ())),
                            precision=lax.Precision.HIGHEST,
                            preferred_element_type=jnp.float32)
        s_scr[pl.ds(t, 1), :] = s[0:1]
        return carry

    lax.fori_loop(0, tile, token, 0)
    coef_ref[...] = gate_ref[...] * _gelu(s_scr[...])


def _peer_v_kernel(idx_ref, coef_ref, h_ref, tab_ref, o_ref, *, tile):
    npair = PEER_HEADS * PEER_TOPK
    nacc = 4

    def token(t, carry):
        accs = [h_ref[t]] + [jnp.zeros((SUBLANES, LANES), jnp.float32)] * (nacc - 1)
        for k in range(npair):
            row = _expert_row(tab_ref, idx_ref[t, k])
            accs[k % nacc] = accs[k % nacc] + coef_ref[t, k] * row
        o_ref[t] = (accs[0] + accs[1]) + (accs[2] + accs[3])
        return carry

    lax.fori_loop(0, tile, token, 0)


def _table_spec(tab):
    return pl.BlockSpec(tab.shape, lambda i: (0, 0, 0), pipeline_mode=pl.Buffered(1))


def _peer_experts(h2, hn2, idx, gate, utab, vtab, tile=ROW_TILE):
    T, D = h2.shape
    npair = idx.shape[1]
    row3 = (tile, SUBLANES, D // SUBLANES)
    hn3 = hn2.reshape(T, SUBLANES, D // SUBLANES)
    h3 = h2.reshape(T, SUBLANES, D // SUBLANES)
    params = pltpu.CompilerParams(dimension_semantics=("arbitrary",),
                                  vmem_limit_bytes=VMEM_TABLE_LIMIT)
    smem_spec = pl.BlockSpec((tile, npair), lambda i: (i, 0), memory_space=pltpu.SMEM)
    coef = pl.pallas_call(
        functools.partial(_peer_u_kernel, tile=tile),
        grid=(T // tile,),
        in_specs=[
            smem_spec,
            pl.BlockSpec(row3, lambda i: (i, 0, 0)),
            pl.BlockSpec((tile, npair), lambda i: (i, 0)),
            _table_spec(utab),
        ],
        out_specs=pl.BlockSpec((tile, npair), lambda i: (i, 0)),
        out_shape=jax.ShapeDtypeStruct((T, npair), jnp.float32),
        scratch_shapes=[pltpu.VMEM((tile, npair), jnp.float32)],
        compiler_params=params,
        name="peer_up",
    )(idx, hn3, gate, utab)
    out3 = pl.pallas_call(
        functools.partial(_peer_v_kernel, tile=tile),
        grid=(T // tile,),
        in_specs=[
            smem_spec,
            smem_spec,
            pl.BlockSpec(row3, lambda i: (i, 0, 0)),
            _table_spec(vtab),
        ],
        out_specs=pl.BlockSpec(row3, lambda i: (i, 0, 0)),
        out_shape=jax.ShapeDtypeStruct(h3.shape, jnp.float32),
        compiler_params=params,
        name="peer_down",
    )(idx, coef, h3, vtab)
    return out3.reshape(T, D)


def _peer_layer(h, gain, wqt_bf, keys, utab, vtab):
    B, Lp, D = h.shape
    h2 = h.reshape(B * Lp, D)
    hn2, idx, gate = _peer_route(h2, gain, wqt_bf, keys)
    return _peer_experts(h2, hn2, idx, gate, utab, vtab).reshape(B, Lp, D)


def _head_rms(x, g_row, nheads):
    dh = x.shape[-1] // nheads
    outs = []
    for hd in range(nheads):
        xh = x[:, hd * dh:(hd + 1) * dh]
        outs.append(_rms(xh, g_row))
    return jnp.concatenate(outs, axis=-1)


def _kv_kernel(h_ref, g_ref, w_ref, kn_ref, k_ref, v_ref, *, valid_rows, tile):
    D = h_ref.shape[-1]
    hn = _rms(h_ref[0], g_ref[...]).astype(jnp.bfloat16)
    kv = jnp.dot(hn, w_ref[...], preferred_element_type=jnp.float32)
    row = pl.program_id(1) * tile + lax.broadcasted_iota(jnp.int32, (tile, 1), 0)
    live = row < valid_rows
    k = _head_rms(kv[:, :D], kn_ref[...], SB_HEADS)
    k_ref[0] = jnp.where(live, k, 0.0).astype(jnp.bfloat16)
    v_ref[0] = jnp.where(live, kv[:, D:], 0.0).astype(jnp.bfloat16)


def _kv_proj(h, gain, w_bf, k_gain, valid_rows):
    B, Lp, D = h.shape
    tile = _seq_tile(Lp)
    dh = D // SB_HEADS
    blk = pl.BlockSpec((1, tile, D), lambda b, i: (b, i, 0))
    return pl.pallas_call(
        functools.partial(_kv_kernel, valid_rows=valid_rows, tile=tile),
        grid=(B, Lp // tile),
        in_specs=[
            blk,
            pl.BlockSpec((1, D), lambda b, i: (0, 0)),
            pl.BlockSpec(w_bf.shape, lambda b, i: (0, 0)),
            pl.BlockSpec((1, dh), lambda b, i: (0, 0)),
        ],
        out_specs=[blk, blk],
        out_shape=[jax.ShapeDtypeStruct(h.shape, jnp.bfloat16)] * 2,
        compiler_params=pltpu.CompilerParams(
            dimension_semantics=("arbitrary", "arbitrary"), vmem_limit_bytes=VMEM_LIMIT),
        name="kv_proj",
    )(h, gain.reshape(1, D), w_bf, k_gain.reshape(1, dh))


def _q_kernel(h_ref, g_ref, w_ref, qn_ref, q_ref):
    hn = _rms(h_ref[0], g_ref[...]).astype(jnp.bfloat16)
    q = jnp.dot(hn, w_ref[...], preferred_element_type=jnp.float32)
    dh = qn_ref.shape[-1]
    q_ref[0] = (_head_rms(q, qn_ref[...], SB_HEADS) * np.float32(dh ** -0.5)).astype(jnp.bfloat16)


def _q_proj(h, gain, w_bf, q_gain):
    B, Lp, D = h.shape
    tile = _seq_tile(Lp)
    dh = D // SB_HEADS
    blk = pl.BlockSpec((1, tile, D), lambda b, i: (b, i, 0))
    return pl.pallas_call(
        _q_kernel,
        grid=(B, Lp // tile),
        in_specs=[
            blk,
            pl.BlockSpec((1, D), lambda b, i: (0, 0)),
            pl.BlockSpec(w_bf.shape, lambda b, i: (0, 0)),
            pl.BlockSpec((1, dh), lambda b, i: (0, 0)),
        ],
        out_specs=blk,
        out_shape=jax.ShapeDtypeStruct(h.shape, jnp.bfloat16),
        compiler_params=pltpu.CompilerParams(
            dimension_semantics=("arbitrary", "arbitrary"), vmem_limit_bytes=VMEM_LIMIT),
        name="q_proj",
    )(h, gain.reshape(1, D), w_bf, q_gain.reshape(1, dh))


def _out_kernel(h_ref, a_ref, w_ref, o_ref):
    o_ref[0] = h_ref[0] + jnp.dot(a_ref[0], w_ref[...], preferred_element_type=jnp.float32)


def _out_proj(h, attn, w_bf):
    B, Lp, D = h.shape
    tile = _seq_tile(Lp)
    blk = pl.BlockSpec((1, tile, D), lambda b, i: (b, i, 0))
    return pl.pallas_call(
        _out_kernel,
        grid=(B, Lp // tile),
        in_specs=[blk, blk, pl.BlockSpec(w_bf.shape, lambda b, i: (0, 0))],
        out_specs=blk,
        out_shape=jax.ShapeDtypeStruct(h.shape, jnp.float32),
        compiler_params=pltpu.CompilerParams(
            dimension_semantics=("arbitrary", "arbitrary"), vmem_limit_bytes=VMEM_LIMIT),
        name="attn_out_proj",
    )(h, attn, w_bf)


def _attn_kernel(q_ref, k_ref, v_ref, o_ref, carry_s, acc_s, *, nblk):
    blk = ROW_TILE
    dh = q_ref.shape[-1] // 2
    r = lax.broadcasted_iota(jnp.int32, (blk, blk), 0)
    c = lax.broadcasted_iota(jnp.int32, (blk, blk), 1)
    tri = (r >= c).astype(jnp.bfloat16)
    tri2 = jnp.concatenate([tri, tri], axis=0)
    causal = c < r

    for hh in range(2):
        cols = slice(hh * dh, (hh + 1) * dh)

        def tile_step(q, j, diag, cols=cols):
            k = k_ref[0, pl.ds(j * blk, blk), cols]
            v = v_ref[0, pl.ds(j * blk, blk), cols]
            z = lax.dot_general(q, k, (((1,), (1,)), ((), ())),
                                preferred_element_type=jnp.float32)
            lk = -(jnp.maximum(z, 0.0) + jnp.log1p(jnp.exp(-jnp.abs(z))))
            if diag:
                lk = jnp.where(causal, lk, 0.0)
            hi = lk.astype(jnp.bfloat16)
            lo = (lk - hi.astype(jnp.float32)).astype(jnp.bfloat16)
            csum = jnp.dot(jnp.concatenate([hi, lo], axis=1), tri2,
                           preferred_element_type=jnp.float32)
            carry = carry_s[...]
            w = jnp.exp(z + csum + carry)
            if diag:
                w = jnp.where(causal, w, 0.0)
            acc_s[...] += jnp.dot(w.astype(jnp.bfloat16), v,
                                  preferred_element_type=jnp.float32)
            carry = carry + csum[:, 0:1]
            carry_s[...] = carry
            return jnp.max(carry)

        def qblock(i, _, cols=cols, tile_step=tile_step):
            q = q_ref[0, pl.ds(i * blk, blk), cols]
            carry_s[...] = jnp.zeros_like(carry_s)
            acc_s[...] = jnp.zeros_like(acc_s)
            mx = tile_step(q, i, True)

            def cond(st):
                return jnp.logical_and(st[0] >= 0, st[1] > STICK_EXIT)

            def body(st):
                return st[0] - 1, tile_step(q, st[0], False)

            lax.while_loop(cond, body, (i - 1, mx))
            o_ref[0, pl.ds(i * blk, blk), cols] = acc_s[...].astype(o_ref.dtype)
            return 0

        lax.fori_loop(0, nblk, qblock, 0)


def _attention(q, k, v):
    B, Lp, D = q.shape
    dh = D // SB_HEADS
    blk = pl.BlockSpec((1, Lp, 2 * dh), lambda b, hp: (b, 0, hp))
    return pl.pallas_call(
        functools.partial(_attn_kernel, nblk=Lp // ROW_TILE),
        grid=(B, SB_HEADS // 2),
        in_specs=[blk, blk, blk],
        out_specs=blk,
        out_shape=jax.ShapeDtypeStruct(q.shape, jnp.bfloat16),
        scratch_shapes=[pltpu.VMEM((ROW_TILE, 1), jnp.float32),
                        pltpu.VMEM((ROW_TILE, dh), jnp.float32)],
        compiler_params=pltpu.CompilerParams(
            dimension_semantics=("arbitrary", "arbitrary"), vmem_limit_bytes=VMEM_LIMIT),
        name="stick_breaking_attn",
    )(q, k, v)


def kernel(x, meta_tokens, norm_mix, norm_ffn, pool_w, pool_scale, kv_norm, w_kv, k_norm,
           w_q, q_norm, w_o, peer_wq, peer_keys, peer_u, peer_v):
    B, S, D = x.shape
    depth = norm_mix.shape[0]
    n_pool = pool_w.shape[0]
    L = N_META + S
    Lp = -(-L // ROW_TILE) * ROW_TILE
    bf = jnp.bfloat16

    meta = jnp.broadcast_to(meta_tokens[None].astype(x.dtype), (B, N_META, D))
    h = jnp.concatenate([meta, x, jnp.zeros((B, Lp - L, D), x.dtype)], axis=1)

    k_sh = v_sh = None
    for layer in range(depth):
        if layer < n_pool:
            h = _pool_layer(h, norm_mix[layer], pool_w[layer].astype(bf), pool_scale[layer])
        else:
            j = layer - n_pool
            q = _q_proj(h, norm_mix[layer], w_q[j].astype(bf), q_norm[j])
            attn = _attention(q, k_sh, v_sh)
            h = _out_proj(h, attn, w_o[j].astype(bf))
        h = _peer_layer(h, norm_ffn[layer], peer_wq[layer].T.astype(bf), peer_keys[layer],
                        _pack_table(peer_u[layer]), _pack_table(peer_v[layer]))
        if layer == n_pool - 1:
            k_sh, v_sh = _kv_proj(h, kv_norm, w_kv.astype(bf), k_norm, L)
    return h[:, N_META:L]
```

```python
import functools

import jax
import jax.numpy as jnp
import numpy as np
from jax import lax
from jax.experimental import pallas as pl
from jax.experimental.pallas import tpu as pltpu

EPS = 1e-6
N_META = 16
POOL_WINDOWS = (2, 4, 8, 16)
SB_HEADS = 16
PEER_HEADS = 8
PEER_TOPK = 16
PEER_N_KEYS = 128

LANES = 128
SUBLANES = 8
ROW_TILE = 128
HALF_ROWS = SUBLANES // 2
ATTN_HEADS_PER_STEP = 4
VMEM_TABLE_LIMIT = 56 * 1024 * 1024
VMEM_LIMIT = 48 * 1024 * 1024
STICK_EXIT = -90.0


def _seq_tile(lp):
    for t in (5 * ROW_TILE, 4 * ROW_TILE, 2 * ROW_TILE):
        if lp % t == 0:
            return t
    return ROW_TILE


def _rms(x, g):
    return x * lax.rsqrt(jnp.mean(x * x, axis=-1, keepdims=True) + EPS) * g


def _pool_kernel(h_ref, halo_ref, g_ref, w_ref, sc_ref, o_ref, hn_s, *, tile):
    i = pl.program_id(1)
    g = g_ref[...]
    h = h_ref[0]
    hn_s[N_META:, :] = _rms(h, g)
    halo = _rms(halo_ref[0], g)
    hn_s[0:N_META, :] = jnp.where(i == 0, 0.0, halo)
    pos = i * tile + lax.broadcasted_iota(jnp.int32, (tile, 1), 0)
    ngrp = len(POOL_WINDOWS)
    gw = h.shape[-1] // ngrp
    for gi, w in enumerate(POOL_WINDOWS):
        cols = slice(gi * gw, (gi + 1) * gw)
        cur = hn_s[N_META:N_META + tile, cols]
        acc = cur
        for k in range(1, w):
            acc = acc + hn_s[N_META - k:N_META - k + tile, cols]
        cnt = jnp.minimum(pos + 1, w).astype(jnp.float32)
        pooled = acc / cnt - cur
        mix = jnp.dot(pooled.astype(jnp.bfloat16), w_ref[gi],
                      preferred_element_type=jnp.float32)
        o_ref[0, :, cols] = h[:, cols] + mix * sc_ref[:, cols]


def _pool_layer(h, gain, w_bf, scale):
    B, Lp, D = h.shape
    tile = _seq_tile(Lp)
    nt = Lp // tile
    hb = tile // N_META
    return pl.pallas_call(
        functools.partial(_pool_kernel, tile=tile),
        grid=(B, nt),
        in_specs=[
            pl.BlockSpec((1, tile, D), lambda b, i: (b, i, 0)),
            pl.BlockSpec((1, N_META, D), lambda b, i: (b, jnp.maximum(i * hb - 1, 0), 0)),
            pl.BlockSpec((1, D), lambda b, i: (0, 0)),
            pl.BlockSpec(w_bf.shape, lambda b, i: (0, 0, 0)),
            pl.BlockSpec((1, D), lambda b, i: (0, 0)),
        ],
        out_specs=pl.BlockSpec((1, tile, D), lambda b, i: (b, i, 0)),
        out_shape=jax.ShapeDtypeStruct(h.shape, jnp.float32),
        scratch_shapes=[pltpu.VMEM((tile + N_META, D), jnp.float32)],
        compiler_params=pltpu.CompilerParams(
            dimension_semantics=("arbitrary", "arbitrary"), vmem_limit_bytes=VMEM_LIMIT),
        name="pool_mixer",
    )(h, h, gain.reshape(1, D), w_bf, scale.reshape(1, D))


def _topk_rows(s, k, index=None):
    if index is None:
        index = lax.broadcasted_iota(jnp.int32, s.shape, 0)
    big = jnp.int32(1 << 20)
    vals, idxs = [], []
    for _ in range(k):
        m = jnp.max(s, axis=0, keepdims=True)
        pos = jnp.min(jnp.where(s == m, index, big), axis=0, keepdims=True)
        vals.append(m)
        idxs.append(pos)
        s = jnp.where(index == pos, -jnp.inf, s)
    return vals, idxs


def _candidates(s1, s2):
    K = len(s1)
    T = s1[0].shape[-1]
    s1_all = jnp.concatenate(s1, axis=0)
    s2_all = jnp.concatenate(s2, axis=0)
    sub = lax.broadcasted_iota(jnp.int32, (SUBLANES, T), 0)
    vals, flat = [], []

    def piece(v, f, nvalid):
        if nvalid < SUBLANES:
            v = jnp.where(sub < nvalid, v, -jnp.inf)
        vals.append(v)
        flat.append(f)

    for b0 in range(0, K, SUBLANES):
        piece(s1[0] + s2_all[b0:b0 + SUBLANES], sub + b0, SUBLANES)
    for a in range(1, SUBLANES):
        piece(s1[a] + s2_all[0:SUBLANES], sub + a * K, min(K // (a + 1), SUBLANES))
    for a0 in range(SUBLANES, K, SUBLANES):
        piece(s1_all[a0:a0 + SUBLANES] + s2[0], (sub + a0) * K, SUBLANES)
    return jnp.concatenate(vals, axis=0), jnp.concatenate(flat, axis=0)


def _route_kernel(h_ref, g_ref, wqt_ref, keys_ref, hn_ref, idx_ref, gate_ref, qt_s):
    K = PEER_TOPK
    hn = _rms(h_ref[...], g_ref[...])
    hn_ref[...] = hn
    qt_s[...] = lax.dot_general(wqt_ref[...], hn.astype(jnp.bfloat16),
                                (((1,), (1,)), ((), ())),
                                preferred_element_type=jnp.float32)
    idx_rows, gate_rows = [], []
    for hd in range(PEER_HEADS):
        tops = []
        for p in range(2):
            r0 = (hd * 2 + p) * LANES
            st = jnp.dot(keys_ref[hd, p].astype(jnp.bfloat16),
                         qt_s[r0:r0 + LANES, :].astype(jnp.bfloat16),
                         preferred_element_type=jnp.float32)
            tops.append(_topk_rows(st, K))
        (s1, i1), (s2, i2) = tops
        cand, flat = _candidates(s1, s2)
        best_s, best_pos = _topk_rows(cand, K, flat)
        bs = jnp.concatenate(best_s, axis=0)
        bp = jnp.concatenate(best_pos, axis=0)
        pa = bp >> 4
        pb = bp & (K - 1)
        e1 = jnp.zeros_like(bp)
        e2 = jnp.zeros_like(bp)
        for a in range(K):
            e1 = jnp.where(pa == a, i1[a], e1)
            e2 = jnp.where(pb == a, i2[a], e2)
        idx_rows.append((e1 * PEER_N_KEYS + e2) * HALF_ROWS)
        ex = jnp.exp(bs - best_s[0])
        gate_rows.append(ex / jnp.sum(ex, axis=0, keepdims=True))
    idx_ref[...] = jnp.concatenate(idx_rows, axis=0)
    gate_ref[...] = jnp.concatenate(gate_rows, axis=0)


def _peer_route(h2, gain, wqt_bf, keys, tile=ROW_TILE):
    T, D = h2.shape
    Q = wqt_bf.shape[0]
    npair = PEER_HEADS * PEER_TOPK
    return pl.pallas_call(
        _route_kernel,
        grid=(T // tile,),
        in_specs=[
            pl.BlockSpec((tile, D), lambda i: (i, 0)),
            pl.BlockSpec((1, D), lambda i: (0, 0)),
            pl.BlockSpec((Q, D), lambda i: (0, 0)),
            pl.BlockSpec(keys.shape, lambda i: (0, 0, 0, 0)),
        ],
        out_specs=[
            pl.BlockSpec((tile, D), lambda i: (i, 0)),
            pl.BlockSpec((npair, tile), lambda i: (0, i)),
            pl.BlockSpec((npair, tile), lambda i: (0, i)),
        ],
        out_shape=[
            jax.ShapeDtypeStruct((T, D), jnp.float32),
            jax.ShapeDtypeStruct((npair, T), jnp.int32),
            jax.ShapeDtypeStruct((npair, T), jnp.float32),
        ],
        scratch_shapes=[pltpu.VMEM((Q, tile), jnp.float32)],
        compiler_params=pltpu.CompilerParams(
            dimension_semantics=("arbitrary",), vmem_limit_bytes=VMEM_LIMIT),
        name="peer_route",
    )(h2, gain.reshape(1, D), wqt_bf, keys)


def _pack_table(tab):
    E, D = tab.shape
    bits = lax.bitcast_convert_type(tab.astype(jnp.bfloat16), jnp.uint16).astype(jnp.uint32)
    packed = bits[:, :D // 2] | (bits[:, D // 2:] << 16)
    return packed.reshape(E * HALF_ROWS, D // SUBLANES)


def _expert_halves(tab_ref, off):
    w = tab_ref[pl.ds(pl.multiple_of(off, HALF_ROWS), HALF_ROWS), :]
    lo = lax.bitcast_convert_type(w << 16, jnp.float32)
    hi = lax.bitcast_convert_type(w & jnp.uint32(0xFFFF0000), jnp.float32)
    return lo, hi


def _fold8(ps):
    sub = lax.broadcasted_iota(jnp.int32, (SUBLANES, LANES), 0)

    def half(x, d, up):
        return x + pltpu.roll(x, (SUBLANES - d) if up else d, 0)

    l1 = [jnp.concatenate([ps[2 * i], ps[2 * i + 1]], axis=0) for i in range(4)]
    l2 = [jnp.where((sub & 2) != 0, half(l1[2 * i], 2, False), half(l1[2 * i + 1], 2, True))
          for i in range(2)]
    return jnp.where((sub & 1) != 0, half(l2[0], 1, False), half(l2[1], 1, True))


def _fold8_order():
    ps = [np.full((HALF_ROWS, 1), float(i)) for i in range(8)]
    sub = np.arange(SUBLANES)[:, None]
    l1 = [np.concatenate([ps[2 * i], ps[2 * i + 1]], axis=0) for i in range(4)]
    l2 = [np.where((sub & 2) != 0, l1[2 * i], l1[2 * i + 1]) for i in range(2)]
    l3 = np.where((sub & 1) != 0, l2[0], l2[1])
    return [int(v) for v in l3[:, 0]]


_FOLD_ORDER = _fold8_order()


def _gelu(x):
    return 0.5 * x * (1.0 + lax.erf(x * np.float32(1.0 / np.sqrt(2.0))))


def _peer_u_kernel(idx_ref, hn_ref, gate_ref, tab_ref, coef_ref, s_scr, *, tile):
    npair = PEER_HEADS * PEER_TOPK
    assert tile == LANES
    ones = jnp.ones((2 * LANES, LANES), jnp.bfloat16)
    lane = lax.broadcasted_iota(jnp.int32, (npair, tile), 1)
    s_scr[...] = jnp.zeros_like(s_scr)

    def lane_sums(part, t):
        hi = part.astype(jnp.bfloat16)
        lo = (part - hi.astype(jnp.float32)).astype(jnp.bfloat16)
        tot = jnp.dot(jnp.concatenate([hi, lo], axis=1), ones,
                      preferred_element_type=jnp.float32)
        s_scr[...] = jnp.where(lane == t, tot, s_scr[...])

    def token(t, prev):
        h = hn_ref[t]
        h_lo, h_hi = h[:HALF_ROWS], h[HALF_ROWS:]
        rows = []
        for g in range(npair // 8):
            fed = [None] * 8
            for r in range(8):
                lo, hi = _expert_halves(tab_ref, idx_ref[g * 8 + r, t])
                fed[_FOLD_ORDER[r]] = lo * h_lo + hi * h_hi
            rows.append(_fold8(fed))
        lane_sums(prev, jnp.maximum(t - 1, 0))
        return jnp.concatenate(rows, axis=0)

    last = lax.fori_loop(0, tile, token, jnp.zeros((npair, LANES), jnp.float32))
    lane_sums(last, tile - 1)
    coef_ref[...] = gate_ref[...] * _gelu(s_scr[...])


def _peer_v_kernel(idx_ref, coef_ref, h_ref, tab_ref, o_ref, *, tile):
    npair = PEER_HEADS * PEER_TOPK
    nacc = 2

    def token(t, carry):
        zero = jnp.zeros((HALF_ROWS, LANES), jnp.float32)
        acc_lo, acc_hi = [zero] * nacc, [zero] * nacc
        for k in range(npair):
            lo, hi = _expert_halves(tab_ref, idx_ref[k, t])
            c = coef_ref[k, t]
            acc_lo[k % nacc] = acc_lo[k % nacc] + c * lo
            acc_hi[k % nacc] = acc_hi[k % nacc] + c * hi
        o_ref[t] = h_ref[t] + jnp.concatenate([acc_lo[0] + acc_lo[1], acc_hi[0] + acc_hi[1]], axis=0)
        return carry

    lax.fori_loop(0, tile, token, 0)


def _table_spec(tab):
    return pl.BlockSpec(tab.shape, lambda i: (0, 0), pipeline_mode=pl.Buffered(1))


def _peer_experts(h2, hn2, idx, gate, utab, vtab, tile=ROW_TILE):
    T, D = h2.shape
    npair = idx.shape[0]
    row3 = (tile, SUBLANES, D // SUBLANES)
    hn3 = hn2.reshape(T, SUBLANES, D // SUBLANES)
    h3 = h2.reshape(T, SUBLANES, D // SUBLANES)
    params = pltpu.CompilerParams(dimension_semantics=("arbitrary",),
                                  vmem_limit_bytes=VMEM_TABLE_LIMIT)
    pair_blk = (npair, tile)
    smem_spec = pl.BlockSpec(pair_blk, lambda i: (0, i), memory_space=pltpu.SMEM)
    coef = pl.pallas_call(
        functools.partial(_peer_u_kernel, tile=tile),
        grid=(T // tile,),
        in_specs=[
            smem_spec,
            pl.BlockSpec(row3, lambda i: (i, 0, 0)),
            pl.BlockSpec(pair_blk, lambda i: (0, i)),
            _table_spec(utab),
        ],
        out_specs=pl.BlockSpec(pair_blk, lambda i: (0, i)),
        out_shape=jax.ShapeDtypeStruct((npair, T), jnp.float32),
        scratch_shapes=[pltpu.VMEM(pair_blk, jnp.float32)],
        compiler_params=params,
        name="peer_up",
    )(idx, hn3, gate, utab)
    out3 = pl.pallas_call(
        functools.partial(_peer_v_kernel, tile=tile),
        grid=(T // tile,),
        in_specs=[
            smem_spec,
            smem_spec,
            pl.BlockSpec(row3, lambda i: (i, 0, 0)),
            _table_spec(vtab),
        ],
        out_specs=pl.BlockSpec(row3, lambda i: (i, 0, 0)),
        out_shape=jax.ShapeDtypeStruct(h3.shape, jnp.float32),
        compiler_params=params,
        name="peer_down",
    )(idx, coef, h3, vtab)
    return out3.reshape(T, D)


def _peer_layer(h, gain, wqt_bf, keys, utab, vtab):
    B, Lp, D = h.shape
    h2 = h.reshape(B * Lp, D)
    hn2, idx, gate = _peer_route(h2, gain, wqt_bf, keys)
    return _peer_experts(h2, hn2, idx, gate, utab, vtab).reshape(B, Lp, D)


def _head_rms(x, g_row, nheads):
    dh = x.shape[-1] // nheads
    outs = []
    for hd in range(nheads):
        xh = x[:, hd * dh:(hd + 1) * dh]
        outs.append(_rms(xh, g_row))
    return jnp.concatenate(outs, axis=-1)


def _kv_kernel(h_ref, g_ref, w_ref, kn_ref, k_ref, v_ref, *, valid_rows, tile):
    D = h_ref.shape[-1]
    hn = _rms(h_ref[0], g_ref[...]).astype(jnp.bfloat16)
    kv = jnp.dot(hn, w_ref[...], preferred_element_type=jnp.float32)
    row = pl.program_id(1) * tile + lax.broadcasted_iota(jnp.int32, (tile, 1), 0)
    live = row < valid_rows
    k = _head_rms(kv[:, :D], kn_ref[...], SB_HEADS)
    k_ref[0] = jnp.where(live, k, 0.0).astype(jnp.bfloat16)
    v_ref[0] = jnp.where(live, kv[:, D:], 0.0).astype(jnp.bfloat16)


def _kv_proj(h, gain, w_bf, k_gain, valid_rows):
    B, Lp, D = h.shape
    tile = _seq_tile(Lp)
    dh = D // SB_HEADS
    blk = pl.BlockSpec((1, tile, D), lambda b, i: (b, i, 0))
    return pl.pallas_call(
        functools.partial(_kv_kernel, valid_rows=valid_rows, tile=tile),
        grid=(B, Lp // tile),
        in_specs=[
            blk,
            pl.BlockSpec((1, D), lambda b, i: (0, 0)),
            pl.BlockSpec(w_bf.shape, lambda b, i: (0, 0)),
            pl.BlockSpec((1, dh), lambda b, i: (0, 0)),
        ],
        out_specs=[blk, blk],
        out_shape=[jax.ShapeDtypeStruct(h.shape, jnp.bfloat16)] * 2,
        compiler_params=pltpu.CompilerParams(
            dimension_semantics=("arbitrary", "arbitrary"), vmem_limit_bytes=VMEM_LIMIT),
        name="kv_proj",
    )(h, gain.reshape(1, D), w_bf, k_gain.reshape(1, dh))


def _q_kernel(h_ref, g_ref, w_ref, qn_ref, q_ref):
    hn = _rms(h_ref[0], g_ref[...]).astype(jnp.bfloat16)
    q = jnp.dot(hn, w_ref[...], preferred_element_type=jnp.float32)
    dh = qn_ref.shape[-1]
    q_ref[0] = (_head_rms(q, qn_ref[...], SB_HEADS) * np.float32(dh ** -0.5)).astype(jnp.bfloat16)


def _q_proj(h, gain, w_bf, q_gain):
    B, Lp, D = h.shape
    tile = _seq_tile(Lp)
    dh = D // SB_HEADS
    blk = pl.BlockSpec((1, tile, D), lambda b, i: (b, i, 0))
    return pl.pallas_call(
        _q_kernel,
        grid=(B, Lp // tile),
        in_specs=[
            blk,
            pl.BlockSpec((1, D), lambda b, i: (0, 0)),
            pl.BlockSpec(w_bf.shape, lambda b, i: (0, 0)),
            pl.BlockSpec((1, dh), lambda b, i: (0, 0)),
        ],
        out_specs=blk,
        out_shape=jax.ShapeDtypeStruct(h.shape, jnp.bfloat16),
        compiler_params=pltpu.CompilerParams(
            dimension_semantics=("arbitrary", "arbitrary"), vmem_limit_bytes=VMEM_LIMIT),
        name="q_proj",
    )(h, gain.reshape(1, D), w_bf, q_gain.reshape(1, dh))


def _out_kernel(h_ref, a_ref, w_ref, o_ref):
    o_ref[0] = h_ref[0] + jnp.dot(a_ref[0], w_ref[...], preferred_element_type=jnp.float32)


def _out_proj(h, attn, w_bf):
    B, Lp, D = h.shape
    tile = _seq_tile(Lp)
    blk = pl.BlockSpec((1, tile, D), lambda b, i: (b, i, 0))
    return pl.pallas_call(
        _out_kernel,
        grid=(B, Lp // tile),
        in_specs=[blk, blk, pl.BlockSpec(w_bf.shape, lambda b, i: (0, 0))],
        out_specs=blk,
        out_shape=jax.ShapeDtypeStruct(h.shape, jnp.float32),
        compiler_params=pltpu.CompilerParams(
            dimension_semantics=("arbitrary", "arbitrary"), vmem_limit_bytes=VMEM_LIMIT),
        name="attn_out_proj",
    )(h, attn, w_bf)


def _attn_kernel(q_ref, k_ref, v_ref, o_ref, carry_s, acc_s, *, nblk, nheads):
    blk = ROW_TILE
    dh = q_ref.shape[-1] // nheads
    r = lax.broadcasted_iota(jnp.int32, (blk, blk), 0)
    c = lax.broadcasted_iota(jnp.int32, (blk, blk), 1)
    tri = (r >= c).astype(jnp.bfloat16)
    tri2 = jnp.concatenate([tri, tri], axis=0)
    causal = c < r

    def tile_step(qs, j, diag):
        worst = None
        for hh in range(nheads):
            cols = slice(hh * dh, (hh + 1) * dh)
            k = k_ref[0, pl.ds(j * blk, blk), cols]
            v = v_ref[0, pl.ds(j * blk, blk), cols]
            z = lax.dot_general(qs[hh], k, (((1,), (1,)), ((), ())),
                                preferred_element_type=jnp.float32)
            lk = -(jnp.maximum(z, 0.0) + jnp.log1p(jnp.exp(-jnp.abs(z))))
            if diag:
                lk = jnp.where(causal, lk, 0.0)
            hi = lk.astype(jnp.bfloat16)
            lo = (lk - hi.astype(jnp.float32)).astype(jnp.bfloat16)
            csum = jnp.dot(jnp.concatenate([hi, lo], axis=1), tri2,
                           preferred_element_type=jnp.float32)
            carry = carry_s[hh]
            w = jnp.exp(z + csum + carry)
            if diag:
                w = jnp.where(causal, w, 0.0)
            acc_s[:, cols] += jnp.dot(w.astype(jnp.bfloat16), v,
                                      preferred_element_type=jnp.float32)
            carry = carry + csum[:, 0:1]
            carry_s[hh] = carry
            worst = carry if worst is None else jnp.maximum(worst, carry)
        return jnp.max(worst)

    def qblock(i, _):
        qs = [q_ref[0, pl.ds(i * blk, blk), hh * dh:(hh + 1) * dh] for hh in range(nheads)]
        carry_s[...] = jnp.zeros_like(carry_s)
        acc_s[...] = jnp.zeros_like(acc_s)
        mx = tile_step(qs, i, True)

        def cond(st):
            return jnp.logical_and(st[0] >= 0, st[1] > STICK_EXIT)

        def body(st):
            return st[0] - 1, tile_step(qs, st[0], False)

        lax.while_loop(cond, body, (i - 1, mx))
        o_ref[0, pl.ds(i * blk, blk), :] = acc_s[...].astype(o_ref.dtype)
        return 0

    lax.fori_loop(0, nblk, qblock, 0)


def _attention(q, k, v):
    B, Lp, D = q.shape
    dh = D // SB_HEADS
    nheads = ATTN_HEADS_PER_STEP
    blk = pl.BlockSpec((1, Lp, nheads * dh), lambda b, hp: (b, 0, hp))
    return pl.pallas_call(
        functools.partial(_attn_kernel, nblk=Lp // ROW_TILE, nheads=nheads),
        grid=(B, SB_HEADS // nheads),
        in_specs=[blk, blk, blk],
        out_specs=blk,
        out_shape=jax.ShapeDtypeStruct(q.shape, jnp.bfloat16),
        scratch_shapes=[pltpu.VMEM((nheads, ROW_TILE, 1), jnp.float32),
                        pltpu.VMEM((ROW_TILE, nheads * dh), jnp.float32)],
        compiler_params=pltpu.CompilerParams(
            dimension_semantics=("arbitrary", "arbitrary"), vmem_limit_bytes=VMEM_LIMIT),
        name="stick_breaking_attn",
    )(q, k, v)


def kernel(x, meta_tokens, norm_mix, norm_ffn, pool_w, pool_scale, kv_norm, w_kv, k_norm,
           w_q, q_norm, w_o, peer_wq, peer_keys, peer_u, peer_v):
    B, S, D = x.shape
    depth = norm_mix.shape[0]
    n_pool = pool_w.shape[0]
    L = N_META + S
    Lp = -(-L // ROW_TILE) * ROW_TILE
    bf = jnp.bfloat16

    meta = jnp.broadcast_to(meta_tokens[None].astype(x.dtype), (B, N_META, D))
    h = jnp.concatenate([meta, x, jnp.zeros((B, Lp - L, D), x.dtype)], axis=1)

    k_sh = v_sh = None
    for layer in range(depth):
        if layer < n_pool:
            h = _pool_layer(h, norm_mix[layer], pool_w[layer].astype(bf), pool_scale[layer])
        else:
            j = layer - n_pool
            q = _q_proj(h, norm_mix[layer], w_q[j].astype(bf), q_norm[j])
            attn = _attention(q, k_sh, v_sh)
            h = _out_proj(h, attn, w_o[j].astype(bf))
        h = _peer_layer(h, norm_ffn[layer], peer_wq[layer].T.astype(bf), peer_keys[layer],
                        _pack_table(peer_u[layer]), _pack_table(peer_v[layer]))
        if layer == n_pool - 1:
            k_sh, v_sh = _kv_proj(h, kv_norm, w_kv.astype(bf), k_norm, L)
    return h[:, N_META:L]
```

```python
import functools

import jax
import jax.numpy as jnp
import numpy as np
from jax import lax
from jax.experimental import pallas as pl
from jax.experimental.pallas import tpu as pltpu

EPS = 1e-6
N_META = 16
POOL_WINDOWS = (2, 4, 8, 16)
SB_HEADS = 16
PEER_HEADS = 8
PEER_TOPK = 16
PEER_N_KEYS = 128

LANES = 128
SUBLANES = 8
ROW_TILE = 128
HALF_ROWS = SUBLANES // 2
ATTN_HEADS_PER_STEP = 4
VMEM_TABLE_LIMIT = 56 * 1024 * 1024
VMEM_LIMIT = 48 * 1024 * 1024
STICK_EXIT = -90.0


def _seq_tile(lp):
    for t in (5 * ROW_TILE, 4 * ROW_TILE, 2 * ROW_TILE):
        if lp % t == 0:
            return t
    return ROW_TILE


def _rms(x, g):
    return x * lax.rsqrt(jnp.mean(x * x, axis=-1, keepdims=True) + EPS) * g


def _pool_kernel(h_ref, halo_ref, g_ref, w_ref, sc_ref, o_ref, hn_s, *, tile):
    i = pl.program_id(1)
    g = g_ref[...]
    h = h_ref[0]
    hn_s[N_META:, :] = _rms(h, g)
    halo = _rms(halo_ref[0], g)
    hn_s[0:N_META, :] = jnp.where(i == 0, 0.0, halo)
    pos = i * tile + lax.broadcasted_iota(jnp.int32, (tile, 1), 0)
    ngrp = len(POOL_WINDOWS)
    gw = h.shape[-1] // ngrp
    for gi, w in enumerate(POOL_WINDOWS):
        cols = slice(gi * gw, (gi + 1) * gw)
        cur = hn_s[N_META:N_META + tile, cols]
        acc = cur
        for k in range(1, w):
            acc = acc + hn_s[N_META - k:N_META - k + tile, cols]
        cnt = jnp.minimum(pos + 1, w).astype(jnp.float32)
        pooled = acc / cnt - cur
        mix = jnp.dot(pooled.astype(jnp.bfloat16), w_ref[gi],
                      preferred_element_type=jnp.float32)
        o_ref[0, :, cols] = h[:, cols] + mix * sc_ref[:, cols]


def _pool_layer(h, gain, w_bf, scale):
    B, Lp, D = h.shape
    tile = _seq_tile(Lp)
    nt = Lp // tile
    hb = tile // N_META
    return pl.pallas_call(
        functools.partial(_pool_kernel, tile=tile),
        grid=(B, nt),
        in_specs=[
            pl.BlockSpec((1, tile, D), lambda b, i: (b, i, 0)),
            pl.BlockSpec((1, N_META, D), lambda b, i: (b, jnp.maximum(i * hb - 1, 0), 0)),
            pl.BlockSpec((1, D), lambda b, i: (0, 0)),
            pl.BlockSpec(w_bf.shape, lambda b, i: (0, 0, 0)),
            pl.BlockSpec((1, D), lambda b, i: (0, 0)),
        ],
        out_specs=pl.BlockSpec((1, tile, D), lambda b, i: (b, i, 0)),
        out_shape=jax.ShapeDtypeStruct(h.shape, jnp.float32),
        scratch_shapes=[pltpu.VMEM((tile + N_META, D), jnp.float32)],
        compiler_params=pltpu.CompilerParams(
            dimension_semantics=("arbitrary", "arbitrary"), vmem_limit_bytes=VMEM_LIMIT),
        name="pool_mixer",
    )(h, h, gain.reshape(1, D), w_bf, scale.reshape(1, D))


def _topk_rows(s, k, index=None):
    if index is None:
        index = lax.broadcasted_iota(jnp.int32, s.shape, 0)
    big = jnp.int32(1 << 20)
    vals, idxs = [], []
    for _ in range(k):
        m = jnp.max(s, axis=0, keepdims=True)
        pos = jnp.min(jnp.where(s == m, index, big), axis=0, keepdims=True)
        vals.append(m)
        idxs.append(pos)
        s = jnp.where(index == pos, -jnp.inf, s)
    return vals, idxs


def _candidates(s1, s2):
    K = len(s1)
    T = s1[0].shape[-1]
    s1_all = jnp.concatenate(s1, axis=0)
    s2_all = jnp.concatenate(s2, axis=0)
    sub = lax.broadcasted_iota(jnp.int32, (SUBLANES, T), 0)
    vals, flat = [], []

    def piece(v, f, nvalid):
        if nvalid < SUBLANES:
            v = jnp.where(sub < nvalid, v, -jnp.inf)
        vals.append(v)
        flat.append(f)

    for b0 in range(0, K, SUBLANES):
        piece(s1[0] + s2_all[b0:b0 + SUBLANES], sub + b0, SUBLANES)
    for a in range(1, SUBLANES):
        piece(s1[a] + s2_all[0:SUBLANES], sub + a * K, min(K // (a + 1), SUBLANES))
    for a0 in range(SUBLANES, K, SUBLANES):
        piece(s1_all[a0:a0 + SUBLANES] + s2[0], (sub + a0) * K, SUBLANES)
    return jnp.concatenate(vals, axis=0), jnp.concatenate(flat, axis=0)


def _route_kernel(h_ref, g_ref, wqt_ref, keys_ref, hn_ref, idx_ref, gate_ref, qt_s):
    K = PEER_TOPK
    hn = _rms(h_ref[...], g_ref[...])
    hn_ref[...] = hn
    qt_s[...] = lax.dot_general(wqt_ref[...], hn.astype(jnp.bfloat16),
                                (((1,), (1,)), ((), ())),
                                preferred_element_type=jnp.float32)
    idx_rows, gate_rows = [], []
    for hd in range(PEER_HEADS):
        tops = []
        for p in range(2):
            r0 = (hd * 2 + p) * LANES
            st = jnp.dot(keys_ref[hd, p].astype(jnp.bfloat16),
                         qt_s[r0:r0 + LANES, :].astype(jnp.bfloat16),
                         preferred_element_type=jnp.float32)
            tops.append(_topk_rows(st, K))
        (s1, i1), (s2, i2) = tops
        cand, flat = _candidates(s1, s2)
        best_s, best_pos = _topk_rows(cand, K, flat)
        bs = jnp.concatenate(best_s, axis=0)
        bp = jnp.concatenate(best_pos, axis=0)
        pa = bp >> 4
        pb = bp & (K - 1)
        e1 = jnp.zeros_like(bp)
        e2 = jnp.zeros_like(bp)
        for a in range(K):
            e1 = jnp.where(pa == a, i1[a], e1)
            e2 = jnp.where(pb == a, i2[a], e2)
        idx_rows.append((e1 * PEER_N_KEYS + e2) * HALF_ROWS)
        ex = jnp.exp(bs - best_s[0])
        gate_rows.append(ex / jnp.sum(ex, axis=0, keepdims=True))
    idx_ref[...] = jnp.concatenate(idx_rows, axis=0).T
    gate_ref[...] = jnp.concatenate(gate_rows, axis=0)


def _peer_route(h2, gain, wqt_bf, keys, tile=ROW_TILE):
    T, D = h2.shape
    Q = wqt_bf.shape[0]
    npair = PEER_HEADS * PEER_TOPK
    return pl.pallas_call(
        _route_kernel,
        grid=(T // tile,),
        in_specs=[
            pl.BlockSpec((tile, D), lambda i: (i, 0)),
            pl.BlockSpec((1, D), lambda i: (0, 0)),
            pl.BlockSpec((Q, D), lambda i: (0, 0)),
            pl.BlockSpec(keys.shape, lambda i: (0, 0, 0, 0)),
        ],
        out_specs=[
            pl.BlockSpec((tile, D), lambda i: (i, 0)),
            pl.BlockSpec((tile, npair), lambda i: (i, 0)),
            pl.BlockSpec((npair, tile), lambda i: (0, i)),
        ],
        out_shape=[
            jax.ShapeDtypeStruct((T, D), jnp.float32),
            jax.ShapeDtypeStruct((T, npair), jnp.int32),
            jax.ShapeDtypeStruct((npair, T), jnp.float32),
        ],
        scratch_shapes=[pltpu.VMEM((Q, tile), jnp.float32)],
        compiler_params=pltpu.CompilerParams(
            dimension_semantics=("arbitrary",), vmem_limit_bytes=VMEM_LIMIT),
        name="peer_route",
    )(h2, gain.reshape(1, D), wqt_bf, keys)


def _pack_table(tab):
    E, D = tab.shape
    bits = lax.bitcast_convert_type(tab.astype(jnp.bfloat16), jnp.uint16).astype(jnp.uint32)
    packed = bits[:, :D // 2] | (bits[:, D // 2:] << 16)
    return packed.reshape(E * HALF_ROWS, D // SUBLANES)


def _expert_halves(tab_ref, off):
    w = tab_ref[pl.ds(pl.multiple_of(off, HALF_ROWS), HALF_ROWS), :]
    lo = lax.bitcast_convert_type(w << 16, jnp.float32)
    hi = lax.bitcast_convert_type(w & jnp.uint32(0xFFFF0000), jnp.float32)
    return lo, hi


def _fold8(ps):
    sub = lax.broadcasted_iota(jnp.int32, (SUBLANES, LANES), 0)

    def half(x, d, up):
        return x + pltpu.roll(x, (SUBLANES - d) if up else d, 0)

    l1 = [jnp.concatenate([ps[2 * i], ps[2 * i + 1]], axis=0) for i in range(4)]
    l2 = [jnp.where((sub & 2) != 0, half(l1[2 * i], 2, False), half(l1[2 * i + 1], 2, True))
          for i in range(2)]
    return jnp.where((sub & 1) != 0, half(l2[0], 1, False), half(l2[1], 1, True))


def _fold8_order():
    ps = [np.full((HALF_ROWS, 1), float(i)) for i in range(8)]
    sub = np.arange(SUBLANES)[:, None]
    l1 = [np.concatenate([ps[2 * i], ps[2 * i + 1]], axis=0) for i in range(4)]
    l2 = [np.where((sub & 2) != 0, l1[2 * i], l1[2 * i + 1]) for i in range(2)]
    l3 = np.where((sub & 1) != 0, l2[0], l2[1])
    return [int(v) for v in l3[:, 0]]


_FOLD_ORDER = _fold8_order()


def _gelu(x):
    return 0.5 * x * (1.0 + lax.erf(x * np.float32(1.0 / np.sqrt(2.0))))


def _peer_u_kernel(idx_ref, hn_ref, gate_ref, tab_ref, coef_ref, s_scr, *, tile):
    npair = PEER_HEADS * PEER_TOPK
    assert tile == LANES
    ones = jnp.ones((2 * LANES, LANES), jnp.bfloat16)
    lane = lax.broadcasted_iota(jnp.int32, (npair, tile), 1)
    s_scr[...] = jnp.zeros_like(s_scr)

    def lane_sums(part, t):
        hi = part.astype(jnp.bfloat16)
        lo = (part - hi.astype(jnp.float32)).astype(jnp.bfloat16)
        tot = jnp.dot(jnp.concatenate([hi, lo], axis=1), ones,
                      preferred_element_type=jnp.float32)
        s_scr[...] = jnp.where(lane == t, tot, s_scr[...])

    def token(t, prev):
        h = hn_ref[t]
        h_lo, h_hi = h[:HALF_ROWS], h[HALF_ROWS:]
        rows = []
        for g in range(npair // 8):
            fed = [None] * 8
            for r in range(8):
                lo, hi = _expert_halves(tab_ref, idx_ref[t, g * 8 + r])
                fed[_FOLD_ORDER[r]] = lo * h_lo + hi * h_hi
            rows.append(_fold8(fed))
        lane_sums(prev, jnp.maximum(t - 1, 0))
        return jnp.concatenate(rows, axis=0)

    last = lax.fori_loop(0, tile, token, jnp.zeros((npair, LANES), jnp.float32))
    lane_sums(last, tile - 1)
    coef_ref[...] = (gate_ref[...] * _gelu(s_scr[...])).T


def _peer_v_kernel(idx_ref, coef_ref, h_ref, tab_ref, o_ref, *, tile):
    npair = PEER_HEADS * PEER_TOPK
    nacc = 2

    def token(t, carry):
        zero = jnp.zeros((HALF_ROWS, LANES), jnp.float32)
        acc_lo, acc_hi = [zero] * nacc, [zero] * nacc
        for k in range(npair):
            lo, hi = _expert_halves(tab_ref, idx_ref[t, k])
            c = coef_ref[t, k]
            acc_lo[k % nacc] = acc_lo[k % nacc] + c * lo
            acc_hi[k % nacc] = acc_hi[k % nacc] + c * hi
        o_ref[t] = h_ref[t] + jnp.concatenate([acc_lo[0] + acc_lo[1], acc_hi[0] + acc_hi[1]], axis=0)
        return carry

    lax.fori_loop(0, tile, token, 0)


def _table_spec(tab):
    return pl.BlockSpec(tab.shape, lambda i: (0, 0), pipeline_mode=pl.Buffered(1))


def _peer_experts(h2, hn2, idx, gate, utab, vtab, tile=ROW_TILE):
    T, D = h2.shape
    npair = idx.shape[1]
    row3 = (tile, SUBLANES, D // SUBLANES)
    hn3 = hn2.reshape(T, SUBLANES, D // SUBLANES)
    h3 = h2.reshape(T, SUBLANES, D // SUBLANES)
    params = pltpu.CompilerParams(dimension_semantics=("arbitrary",),
                                  vmem_limit_bytes=VMEM_TABLE_LIMIT)
    pair_blk = (npair, tile)
    smem_spec = pl.BlockSpec((tile, npair), lambda i: (i, 0), memory_space=pltpu.SMEM)
    coef = pl.pallas_call(
        functools.partial(_peer_u_kernel, tile=tile),
        grid=(T // tile,),
        in_specs=[
            smem_spec,
            pl.BlockSpec(row3, lambda i: (i, 0, 0)),
            pl.BlockSpec(pair_blk, lambda i: (0, i)),
            _table_spec(utab),
        ],
        out_specs=pl.BlockSpec((tile, npair), lambda i: (i, 0)),
        out_shape=jax.ShapeDtypeStruct((T, npair), jnp.float32),
        scratch_shapes=[pltpu.VMEM(pair_blk, jnp.float32)],
        compiler_params=params,
        name="peer_up",
    )(idx, hn3, gate, utab)
    out3 = pl.pallas_call(
        functools.partial(_peer_v_kernel, tile=tile),
        grid=(T // tile,),
        in_specs=[
            smem_spec,
            smem_spec,
            pl.BlockSpec(row3, lambda i: (i, 0, 0)),
            _table_spec(vtab),
        ],
        out_specs=pl.BlockSpec(row3, lambda i: (i, 0, 0)),
        out_shape=jax.ShapeDtypeStruct(h3.shape, jnp.float32),
        compiler_params=params,
        name="peer_down",
    )(idx, coef, h3, vtab)
    return out3.reshape(T, D)


def _peer_layer(h, gain, wqt_bf, keys, utab, vtab):
    B, Lp, D = h.shape
    h2 = h.reshape(B * Lp, D)
    hn2, idx, gate = _peer_route(h2, gain, wqt_bf, keys)
    return _peer_experts(h2, hn2, idx, gate, utab, vtab).reshape(B, Lp, D)


def _head_rms(x, g_row, nheads):
    dh = x.shape[-1] // nheads
    outs = []
    for hd in range(nheads):
        xh = x[:, hd * dh:(hd + 1) * dh]
        outs.append(_rms(xh, g_row))
    return jnp.concatenate(outs, axis=-1)


def _kv_kernel(h_ref, g_ref, w_ref, kn_ref, k_ref, v_ref, *, valid_rows, tile):
    D = h_ref.shape[-1]
    hn = _rms(h_ref[0], g_ref[...]).astype(jnp.bfloat16)
    kv = jnp.dot(hn, w_ref[...], preferred_element_type=jnp.float32)
    row = pl.program_id(1) * tile + lax.broadcasted_iota(jnp.int32, (tile, 1), 0)
    live = row < valid_rows
    k = _head_rms(kv[:, :D], kn_ref[...], SB_HEADS)
    k_ref[0] = jnp.where(live, k, 0.0).astype(jnp.bfloat16)
    v_ref[0] = jnp.where(live, kv[:, D:], 0.0).astype(jnp.bfloat16)


def _kv_proj(h, gain, w_bf, k_gain, valid_rows):
    B, Lp, D = h.shape
    tile = _seq_tile(Lp)
    dh = D // SB_HEADS
    blk = pl.BlockSpec((1, tile, D), lambda b, i: (b, i, 0))
    return pl.pallas_call(
        functools.partial(_kv_kernel, valid_rows=valid_rows, tile=tile),
        grid=(B, Lp // tile),
        in_specs=[
            blk,
            pl.BlockSpec((1, D), lambda b, i: (0, 0)),
            pl.BlockSpec(w_bf.shape, lambda b, i: (0, 0)),
            pl.BlockSpec((1, dh), lambda b, i: (0, 0)),
        ],
        out_specs=[blk, blk],
        out_shape=[jax.ShapeDtypeStruct(h.shape, jnp.bfloat16)] * 2,
        compiler_params=pltpu.CompilerParams(
            dimension_semantics=("arbitrary", "arbitrary"), vmem_limit_bytes=VMEM_LIMIT),
        name="kv_proj",
    )(h, gain.reshape(1, D), w_bf, k_gain.reshape(1, dh))


def _q_kernel(h_ref, g_ref, w_ref, qn_ref, q_ref):
    hn = _rms(h_ref[0], g_ref[...]).astype(jnp.bfloat16)
    q = jnp.dot(hn, w_ref[...], preferred_element_type=jnp.float32)
    dh = qn_ref.shape[-1]
    q_ref[0] = (_head_rms(q, qn_ref[...], SB_HEADS) * np.float32(dh ** -0.5)).astype(jnp.bfloat16)


def _q_proj(h, gain, w_bf, q_gain):
    B, Lp, D = h.shape
    tile = _seq_tile(Lp)
    dh = D // SB_HEADS
    blk = pl.BlockSpec((1, tile, D), lambda b, i: (b, i, 0))
    return pl.pallas_call(
        _q_kernel,
        grid=(B, Lp // tile),
        in_specs=[
            blk,
            pl.BlockSpec((1, D), lambda b, i: (0, 0)),
            pl.BlockSpec(w_bf.shape, lambda b, i: (0, 0)),
            pl.BlockSpec((1, dh), lambda b, i: (0, 0)),
        ],
        out_specs=blk,
        out_shape=jax.ShapeDtypeStruct(h.shape, jnp.bfloat16),
        compiler_params=pltpu.CompilerParams(
            dimension_semantics=("arbitrary", "arbitrary"), vmem_limit_bytes=VMEM_LIMIT),
        name="q_proj",
    )(h, gain.reshape(1, D), w_bf, q_gain.reshape(1, dh))


def _out_kernel(h_ref, a_ref, w_ref, o_ref):
    o_ref[0] = h_ref[0] + jnp.dot(a_ref[0], w_ref[...], preferred_element_type=jnp.float32)


def _out_proj(h, attn, w_bf):
    B, Lp, D = h.shape
    tile = _seq_tile(Lp)
    blk = pl.BlockSpec((1, tile, D), lambda b, i: (b, i, 0))
    return pl.pallas_call(
        _out_kernel,
        grid=(B, Lp // tile),
        in_specs=[blk, blk, pl.BlockSpec(w_bf.shape, lambda b, i: (0, 0))],
        out_specs=blk,
        out_shape=jax.ShapeDtypeStruct(h.shape, jnp.float32),
        compiler_params=pltpu.CompilerParams(
            dimension_semantics=("arbitrary", "arbitrary"), vmem_limit_bytes=VMEM_LIMIT),
        name="attn_out_proj",
    )(h, attn, w_bf)


def _attn_kernel(q_ref, k_ref, v_ref, o_ref, carry_s, acc_s, *, nblk, nheads):
    blk = ROW_TILE
    dh = q_ref.shape[-1] // nheads
    r = lax.broadcasted_iota(jnp.int32, (blk, blk), 0)
    c = lax.broadcasted_iota(jnp.int32, (blk, blk), 1)
    tri = (r >= c).astype(jnp.bfloat16)
    tri2 = jnp.concatenate([tri, tri], axis=0)
    causal = c < r

    def tile_step(qs, j, diag):
        worst = None
        for hh in range(nheads):
            cols = slice(hh * dh, (hh + 1) * dh)
            k = k_ref[0, pl.ds(j * blk, blk), cols]
            v = v_ref[0, pl.ds(j * blk, blk), cols]
            z = lax.dot_general(qs[hh], k, (((1,), (1,)), ((), ())),
                                preferred_element_type=jnp.float32)
            lk = -(jnp.maximum(z, 0.0) + jnp.log1p(jnp.exp(-jnp.abs(z))))
            if diag:
                lk = jnp.where(causal, lk, 0.0)
            hi = lk.astype(jnp.bfloat16)
            lo = (lk - hi.astype(jnp.float32)).astype(jnp.bfloat16)
            csum = jnp.dot(jnp.concatenate([hi, lo], axis=1), tri2,
                           preferred_element_type=jnp.float32)
            carry = carry_s[hh]
            w = jnp.exp(z + csum + carry)
            if diag:
                w = jnp.where(causal, w, 0.0)
            acc_s[:, cols] += jnp.dot(w.astype(jnp.bfloat16), v,
                                      preferred_element_type=jnp.float32)
            carry = carry + csum[:, 0:1]
            carry_s[hh] = carry
            worst = carry if worst is None else jnp.maximum(worst, carry)
        return jnp.max(worst)

    def qblock(i, _):
        qs = [q_ref[0, pl.ds(i * blk, blk), hh * dh:(hh + 1) * dh] for hh in range(nheads)]
        carry_s[...] = jnp.zeros_like(carry_s)
        acc_s[...] = jnp.zeros_like(acc_s)
        mx = tile_step(qs, i, True)

        def cond(st):
            return jnp.logical_and(st[0] >= 0, st[1] > STICK_EXIT)

        def body(st):
            return st[0] - 1, tile_step(qs, st[0], False)

        lax.while_loop(cond, body, (i - 1, mx))
        o_ref[0, pl.ds(i * blk, blk), :] = acc_s[...].astype(o_ref.dtype)
        return 0

    lax.fori_loop(0, nblk, qblock, 0)


def _attention(q, k, v):
    B, Lp, D = q.shape
    dh = D // SB_HEADS
    nheads = ATTN_HEADS_PER_STEP
    blk = pl.BlockSpec((1, Lp, nheads * dh), lambda b, hp: (b, 0, hp))
    return pl.pallas_call(
        functools.partial(_attn_kernel, nblk=Lp // ROW_TILE, nheads=nheads),
        grid=(B, SB_HEADS // nheads),
        in_specs=[blk, blk, blk],
        out_specs=blk,
        out_shape=jax.ShapeDtypeStruct(q.shape, jnp.bfloat16),
        scratch_shapes=[pltpu.VMEM((nheads, ROW_TILE, 1), jnp.float32),
                        pltpu.VMEM((ROW_TILE, nheads * dh), jnp.float32)],
        compiler_params=pltpu.CompilerParams(
            dimension_semantics=("arbitrary", "arbitrary"), vmem_limit_bytes=VMEM_LIMIT),
        name="stick_breaking_attn",
    )(q, k, v)


def kernel(x, meta_tokens, norm_mix, norm_ffn, pool_w, pool_scale, kv_norm, w_kv, k_norm,
           w_q, q_norm, w_o, peer_wq, peer_keys, peer_u, peer_v):
    B, S, D = x.shape
    depth = norm_mix.shape[0]
    n_pool = pool_w.shape[0]
    L = N_META + S
    Lp = -(-L // ROW_TILE) * ROW_TILE
    bf = jnp.bfloat16

    meta = jnp.broadcast_to(meta_tokens[None].astype(x.dtype), (B, N_META, D))
    h = jnp.concatenate([meta, x, jnp.zeros((B, Lp - L, D), x.dtype)], axis=1)

    k_sh = v_sh = None
    for layer in range(depth):
        if layer < n_pool:
            h = _pool_layer(h, norm_mix[layer], pool_w[layer].astype(bf), pool_scale[layer])
        else:
            j = layer - n_pool
            q = _q_proj(h, norm_mix[layer], w_q[j].astype(bf), q_norm[j])
            attn = _attention(q, k_sh, v_sh)
            h = _out_proj(h, attn, w_o[j].astype(bf))
        h = _peer_layer(h, norm_ffn[layer], peer_wq[layer].T.astype(bf), peer_keys[layer],
                        _pack_table(peer_u[layer]), _pack_table(peer_v[layer]))
        if layer == n_pool - 1:
            k_sh, v_sh = _kv_proj(h, kv_norm, w_kv.astype(bf), k_norm, L)
    return h[:, N_META:L]
```
